```python
import jax
import jax.numpy as jnp
from jax import lax
import numpy as np

D_MODEL = 1024
BATCH = 4
SEQ = 4096
DEPTH = 2
DEC_BATCH = 128
DEC_SEQ = 8
PAST_LEN = 16384
PAGE_SIZE = 128

N_HEADS = 16
N_KV_HEADS = 4
HEAD_DIM = 64
GROUP = N_HEADS // N_KV_HEADS
ATTN_SCALE = HEAD_DIM ** -0.5
WINDOW = 128
MOBA_BLOCK = 256
MOBA_TOPK = 3
MOBA_Q_CHUNK = 32
MEM_LEN = 256
CROSS_HEADS = 4
CROSS_HEAD_DIM = D_MODEL // CROSS_HEADS
CROSS_SCALE = CROSS_HEAD_DIM ** -0.5
D_FF = 2816
N_EXPERTS = 8
EXPERT_TOPK = 2
D_FF_EXPERT = 3584
ROPE_THETA = 10000.0
RMS_EPS = 1e-5
N_WIN_LAYERS = (DEPTH + 1) // 2
N_MOBA_LAYERS = DEPTH // 2
QKV_WIDTH = (N_HEADS + 2 * N_KV_HEADS) * HEAD_DIM

kernel_name = 'hybrid_swa_sink_moba_decoder_step'


def _rms(x, g):
    xf = x.astype(jnp.float32)
    y = xf * lax.rsqrt(jnp.mean(xf * xf, axis=-1, keepdims=True) + RMS_EPS)
    return (y * g.astype(jnp.float32)).astype(x.dtype)


def _rope(x, pos):
    half = HEAD_DIM // 2
    inv_freq = ROPE_THETA ** (-jnp.arange(half, dtype=jnp.float32) / half)
    ang = pos.astype(jnp.float32)[:, None] * inv_freq[None, :]
    cos = jnp.cos(ang)[None, :, None, :]
    sin = jnp.sin(ang)[None, :, None, :]
    xf = x.astype(jnp.float32)
    x1, x2 = xf[..., :half], xf[..., half:]
    return jnp.concatenate([x1 * cos - x2 * sin, x2 * cos + x1 * sin], axis=-1).astype(x.dtype)


def _self_qkv(xn, w_qkv, pos):
    n, t, _ = xn.shape
    proj = jnp.einsum('ntd,de->nte', xn, w_qkv)
    q_end = N_HEADS * HEAD_DIM
    k_end = q_end + N_KV_HEADS * HEAD_DIM
    q = _rope(proj[..., :q_end].reshape(n, t, N_HEADS, HEAD_DIM), pos)
    k = _rope(proj[..., q_end:k_end].reshape(n, t, N_KV_HEADS, HEAD_DIM), pos)
    v = proj[..., k_end:].reshape(n, t, N_KV_HEADS, HEAD_DIM)
    return q, k, v


def _sink_probs(s, sink):
    m = jnp.maximum(jnp.max(s, axis=-1, keepdims=True), sink)
    e = jnp.exp(s - m)
    return e / (jnp.sum(e, axis=-1, keepdims=True) + jnp.exp(sink - m))


def _window_prompt(q, k, v, sink):
    b, s = q.shape[:2]
    nb = s // WINDOW
    qb = q.reshape(b, nb, WINDOW, N_KV_HEADS, GROUP, HEAD_DIM)
    kb = k.reshape(b, nb, WINDOW, N_KV_HEADS, HEAD_DIM)
    vb = v.reshape(b, nb, WINDOW, N_KV_HEADS, HEAD_DIM)
    shift = ((0, 0), (1, 0), (0, 0), (0, 0), (0, 0))
    kk = jnp.concatenate([jnp.pad(kb, shift)[:, :-1], kb], axis=2)
    vv = jnp.concatenate([jnp.pad(vb, shift)[:, :-1], vb], axis=2)
    sc = jnp.einsum('bnqkgd,bnlkd->bnkgql', qb, kk).astype(jnp.float32) * ATTN_SCALE
    qi = jnp.arange(WINDOW)[:, None]
    li = jnp.arange(2 * WINDOW)[None, :]
    band = (li > qi) & (li <= qi + WINDOW)
    has_prev = (jnp.arange(nb) > 0)[:, None, None] | (li >= WINDOW)[None]
    ok = band[None] & has_prev
    sc = jnp.where(ok[None, :, None, None], sc, -jnp.inf)
    sk = sink.astype(jnp.float32).reshape(N_KV_HEADS, GROUP)[None, None, :, :, None, None]
    p = _sink_probs(sc, sk).astype(vv.dtype)
    o = jnp.einsum('bnkgql,bnlkd->bnqkgd', p, vv)
    return o.reshape(b, s, N_HEADS * HEAD_DIM)


def _window_sample(q, k_new, v_new, k_buf, v_buf, sink, past_len):
    n, t = q.shape[:2]
    nbuf = k_buf.shape[1]
    kk = jnp.concatenate([k_buf, k_new], axis=1)
    vv = jnp.concatenate([v_buf, v_new], axis=1)
    qpos = past_len + jnp.arange(t)
    kpos = past_len - nbuf + jnp.arange(nbuf + t)
    ok = (kpos[None, :] <= qpos[:, None]) & (kpos[None, :] > qpos[:, None] - WINDOW)
    qg = q.reshape(n, t, N_KV_HEADS, GROUP, HEAD_DIM)
    sc = jnp.einsum('ntkgd,nlkd->nkgtl', qg, kk).astype(jnp.float32) * ATTN_SCALE
    sc = jnp.where(ok, sc, -jnp.inf)
    sk = sink.astype(jnp.float32).reshape(N_KV_HEADS, GROUP)[None, :, :, None, None]
    p = _sink_probs(sc, sk).astype(vv.dtype)
    o = jnp.einsum('nkgtl,nlkd->ntkgd', p, vv).reshape(n, t, N_HEADS * HEAD_DIM)
    return o, kk[:, -nbuf:], vv[:, -nbuf:]


def _group_q(q):
    n, t = q.shape[:2]
    return (q * ATTN_SCALE).reshape(n, t, N_KV_HEADS, GROUP, HEAD_DIM).transpose(0, 2, 3, 1, 4)


def _moba_attend(q, k_own, v_own, own_ok, k_sel=None, v_sel=None, sel_ok=None):
    s_own = jnp.einsum('nkgtd,nlkd->nkgtl', q, k_own).astype(jnp.float32)
    s_own = jnp.where(own_ok, s_own, -jnp.inf)
    if k_sel is None:
        p = jax.nn.softmax(s_own, axis=-1).astype(v_own.dtype)
        return jnp.einsum('nkgtl,nlkd->nkgtd', p, v_own)
    s_sel = jnp.einsum('nkgtd,nkgtrjd->nkgtrj', q, k_sel).astype(jnp.float32)
    s_sel = jnp.where(sel_ok[..., None], s_sel, -jnp.inf)
    n, kv, g, t, r, j = s_sel.shape
    s = jnp.concatenate([s_sel.reshape(n, kv, g, t, r * j), s_own], axis=-1)
    p = jax.nn.softmax(s, axis=-1).astype(v_own.dtype)
    p_sel = p[..., :r * j].reshape(n, kv, g, t, r, j)
    p_own = p[..., r * j:]
    return (jnp.einsum('nkgtrj,nkgtrjd->nkgtd', p_sel, v_sel)
            + jnp.einsum('nkgtl,nlkd->nkgtd', p_own, v_own))


def _moba_prompt(q, k, v):
    b, s = q.shape[:2]
    nb = -(-s // MOBA_BLOCK)
    pad = nb * MOBA_BLOCK - s
    kp = jnp.pad(k, ((0, 0), (0, pad), (0, 0), (0, 0)))
    vp = jnp.pad(v, ((0, 0), (0, pad), (0, 0), (0, 0)))
    kb = kp.reshape(b, nb, MOBA_BLOCK, N_KV_HEADS, HEAD_DIM)
    vb = vp.reshape(b, nb, MOBA_BLOCK, N_KV_HEADS, HEAD_DIM)
    qg = _group_q(q)
    own = jnp.arange(s) // MOBA_BLOCK
    r = min(MOBA_TOPK, nb - 1)
    if r > 0:
        kmean = jnp.mean(kb.astype(jnp.float32), axis=2)
        gate = jnp.einsum('bkgsd,bnkd->bkgsn', qg.astype(jnp.float32), kmean)
        gate = jnp.where(jnp.arange(nb)[None, :] < own[:, None], gate, -jnp.inf)
        idx = lax.top_k(gate, r)[1]
        sel_ok = jnp.arange(r)[None, :] < own[:, None]
        bi = jnp.arange(b).reshape(b, 1, 1, 1, 1)
        ki = jnp.arange(N_KV_HEADS).reshape(1, N_KV_HEADS, 1, 1, 1)

    def chunk(c):
        t0 = c * MOBA_Q_CHUNK
        qc = lax.dynamic_slice_in_dim(qg, t0, MOBA_Q_CHUNK, axis=3)
        blk = t0 // MOBA_BLOCK
        k_own = lax.dynamic_slice_in_dim(kp, blk * MOBA_BLOCK, MOBA_BLOCK, axis=1)
        v_own = lax.dynamic_slice_in_dim(vp, blk * MOBA_BLOCK, MOBA_BLOCK, axis=1)
        qpos = t0 + jnp.arange(MOBA_Q_CHUNK)
        kpos = blk * MOBA_BLOCK + jnp.arange(MOBA_BLOCK)
        own_ok = kpos[None, :] <= qpos[:, None]
        if r == 0:
            return _moba_attend(qc, k_own, v_own, own_ok)
        ic = lax.dynamic_slice_in_dim(idx, t0, MOBA_Q_CHUNK, axis=3)
        k_sel = kb[bi, ic, :, ki, :]
        v_sel = vb[bi, ic, :, ki, :]
        ok = lax.dynamic_slice_in_dim(sel_ok, t0, MOBA_Q_CHUNK, axis=0)
        return _moba_attend(qc, k_own, v_own, own_ok, k_sel, v_sel, ok)

    o = lax.map(chunk, jnp.arange(s // MOBA_Q_CHUNK))
    return o.transpose(1, 0, 4, 2, 3, 5).reshape(b, s, N_HEADS * HEAD_DIM)


def _moba_sample(q, k_new, v_new, k_pool, v_pool, layer, page_table):
    n, t = q.shape[:2]
    n_pages = page_table.shape[1]
    past = n_pages * PAGE_SIZE
    ppb = MOBA_BLOCK // PAGE_SIZE
    k_past = k_pool[layer, page_table].reshape(n, past, N_KV_HEADS, HEAD_DIM)
    tail = min(MOBA_BLOCK, past)
    v_tail = v_pool[layer, page_table[:, n_pages - tail // PAGE_SIZE:]].reshape(n, tail, N_KV_HEADS, HEAD_DIM)
    k_own = jnp.concatenate([k_past[:, past - tail:], k_new], axis=1)
    v_own = jnp.concatenate([v_tail, v_new], axis=1)
    qpos = past + jnp.arange(t)
    own = qpos // MOBA_BLOCK
    kpos = past - tail + jnp.arange(tail + t)
    own_ok = (kpos[None, :] >= (own * MOBA_BLOCK)[:, None]) & (kpos[None, :] <= qpos[:, None])
    qg = _group_q(q)
    nbp = past // MOBA_BLOCK
    r = min(MOBA_TOPK, nbp)
    if r > 0:
        kmean = jnp.mean(k_past[:, :nbp * MOBA_BLOCK].reshape(n, nbp, MOBA_BLOCK, N_KV_HEADS, HEAD_DIM).astype(jnp.float32), axis=2)
        gate = jnp.einsum('nkgtd,nbkd->nkgtb', qg.astype(jnp.float32), kmean)
        gate = jnp.where(jnp.arange(nbp)[None, :] < own[:, None], gate, -jnp.inf)
        idx = lax.top_k(gate, r)[1]
        sel_ok = jnp.arange(r)[None, :] < own[:, None]
        ni = jnp.arange(n).reshape(n, 1, 1, 1, 1, 1)
        ki = jnp.arange(N_KV_HEADS).reshape(1, N_KV_HEADS, 1, 1, 1, 1)
        sel_shape = (n, N_KV_HEADS, GROUP, 1, r, MOBA_BLOCK, HEAD_DIM)

    def one_query(j):
        qt = lax.dynamic_slice_in_dim(qg, j, 1, axis=3)
        ok_own = lax.dynamic_slice_in_dim(own_ok, j, 1, axis=0)
        if r == 0:
            return _moba_attend(qt, k_own, v_own, ok_own)[:, :, :, 0]
        it = lax.dynamic_slice_in_dim(idx, j, 1, axis=3)
        phys = page_table[ni, it[..., None] * ppb + jnp.arange(ppb)]
        k_sel = k_pool[layer, phys, :, ki, :].reshape(sel_shape)
        v_sel = v_pool[layer, phys, :, ki, :].reshape(sel_shape)
        ok_sel = lax.dynamic_slice_in_dim(sel_ok, j, 1, axis=0)
        return _moba_attend(qt, k_own, v_own, ok_own, k_sel, v_sel, ok_sel)[:, :, :, 0]

    o = lax.map(one_query, jnp.arange(t))
    return o.transpose(1, 0, 2, 3, 4).reshape(n, t, N_HEADS * HEAD_DIM)


def _cross_kv(mem, w_ckv):
    n, m, _ = mem.shape
    kv = jnp.einsum('nmd,de->nme', mem, w_ckv)
    w = CROSS_HEADS * CROSS_HEAD_DIM
    return (kv[..., :w].reshape(n, m, CROSS_HEADS, CROSS_HEAD_DIM),
            kv[..., w:].reshape(n, m, CROSS_HEADS, CROSS_HEAD_DIM))


def _cross_attn(xn, mk, mv, w_cq, w_co):
    n, t, _ = xn.shape
    q = jnp.einsum('ntd,de->nte', xn, w_cq).reshape(n, t, CROSS_HEADS, CROSS_HEAD_DIM)
    sc = jnp.einsum('nthd,nmhd->nhtm', q, mk).astype(jnp.float32) * CROSS_SCALE
    p = jax.nn.softmax(sc, axis=-1).astype(mv.dtype)
    o = jnp.einsum('nhtm,nmhd->nthd', p, mv).reshape(n, t, CROSS_HEADS * CROSS_HEAD_DIM)
    return jnp.einsum('nte,ed->ntd', o, w_co)


def _swiglu(x, w_gu, w_down):
    gu = jnp.einsum('ntd,df->ntf', x, w_gu)
    f = w_down.shape[0]
    return jnp.einsum('ntf,fd->ntd', jax.nn.silu(gu[..., :f]) * gu[..., f:], w_down)


def _moe(x, w_router, b_router, w_gu, w_down):
    logits = jnp.einsum('ntd,de->nte', x, w_router).astype(jnp.float32) + b_router.astype(jnp.float32)
    top_val, top_idx = lax.top_k(logits, EXPERT_TOPK)
    gates = jax.nn.softmax(top_val, axis=-1)
    combine = jnp.sum(gates[..., None] * jax.nn.one_hot(top_idx, N_EXPERTS, dtype=jnp.float32), axis=-2)
    y = jnp.zeros_like(x)
    for e in range(N_EXPERTS):
        y = y + combine[..., e:e + 1].astype(x.dtype) * _swiglu(x, w_gu[e], w_down[e])
    return y


def setup_inputs(seed: int = 0) -> dict:
    key = jax.random.key(seed)
    ks = jax.random.split(key, 26)
    f32 = jnp.float32
    n_pages = PAST_LEN // PAGE_SIZE
    n_phys = (DEC_BATCH * n_pages * 5) // 4
    n_win_buf = min(WINDOW, PAST_LEN)

    def nrm(k, shape, scale=1.0):
        return jax.random.normal(k, shape, f32) * scale

    page_table = jax.random.permutation(ks[6], n_phys)[:DEC_BATCH * n_pages].reshape(DEC_BATCH, n_pages).astype(jnp.int32)
    return {
        'x_prompt': nrm(ks[0], (BATCH, SEQ, D_MODEL)),
        'x_sample': nrm(ks[1], (DEC_BATCH, DEC_SEQ, D_MODEL)),
        'cache_win_k': nrm(ks[2], (N_WIN_LAYERS, DEC_BATCH, n_win_buf, N_KV_HEADS, HEAD_DIM)),
        'cache_win_v': nrm(ks[3], (N_WIN_LAYERS, DEC_BATCH, n_win_buf, N_KV_HEADS, HEAD_DIM)),
        'cache_moba_k': nrm(ks[4], (N_MOBA_LAYERS, n_phys, PAGE_SIZE, N_KV_HEADS, HEAD_DIM)),
        'cache_moba_v': nrm(ks[5], (N_MOBA_LAYERS, n_phys, PAGE_SIZE, N_KV_HEADS, HEAD_DIM)),
        'page_table': page_table,
        'cache_mem_k': nrm(ks[7], (DEPTH, DEC_BATCH, MEM_LEN, CROSS_HEADS, CROSS_HEAD_DIM)),
        'cache_mem_v': nrm(ks[8], (DEPTH, DEC_BATCH, MEM_LEN, CROSS_HEADS, CROSS_HEAD_DIM)),
        'mem_prompt': nrm(ks[9], (BATCH, MEM_LEN, D_MODEL)),
        'norm_attn': 1.0 + nrm(ks[10], (DEPTH, D_MODEL), 0.02),
        'norm_cross': 1.0 + nrm(ks[11], (DEPTH, D_MODEL), 0.02),
        'norm_ffn': 1.0 + nrm(ks[12], (DEPTH, D_MODEL), 0.02),
        'norm_final': 1.0 + nrm(ks[13], (D_MODEL,), 0.02),
        'w_qkv': nrm(ks[14], (DEPTH, D_MODEL, QKV_WIDTH), D_MODEL ** -0.5),
        'w_o': nrm(ks[15], (DEPTH, N_HEADS * HEAD_DIM, D_MODEL), (N_HEADS * HEAD_DIM) ** -0.5),
        'sinks': nrm(ks[16], (N_WIN_LAYERS, N_HEADS)),
        'w_cq': nrm(ks[17], (DEPTH, D_MODEL, CROSS_HEADS * CROSS_HEAD_DIM), D_MODEL ** -0.5),
        'w_ckv': nrm(ks[18], (DEPTH, D_MODEL, 2 * CROSS_HEADS * CROSS_HEAD_DIM), D_MODEL ** -0.5),
        'w_co': nrm(ks[19], (DEPTH, CROSS_HEADS * CROSS_HEAD_DIM, D_MODEL), (CROSS_HEADS * CROSS_HEAD_DIM) ** -0.5),
        'w_ffn_gu': nrm(ks[20], (N_WIN_LAYERS, D_MODEL, 2 * D_FF), D_MODEL ** -0.5),
        'w_ffn_down': nrm(ks[21], (N_WIN_LAYERS, D_FF, D_MODEL), D_FF ** -0.5),
        'w_router': nrm(ks[22], (N_MOBA_LAYERS, D_MODEL, N_EXPERTS), D_MODEL ** -0.5),
        'b_router': nrm(ks[23], (N_MOBA_LAYERS, N_EXPERTS), 0.01),
        'w_exp_gu': nrm(ks[24], (N_MOBA_LAYERS, N_EXPERTS, D_MODEL, 2 * D_FF_EXPERT), D_MODEL ** -0.5),
        'w_exp_down': nrm(ks[25], (N_MOBA_LAYERS, N_EXPERTS, D_FF_EXPERT, D_MODEL), D_FF_EXPERT ** -0.5),
    }


def reference(x_prompt, x_sample, cache_win_k, cache_win_v, cache_moba_k, cache_moba_v, page_table,
              cache_mem_k, cache_mem_v, mem_prompt, norm_attn, norm_cross, norm_ffn, norm_final,
              w_qkv, w_o, sinks, w_cq, w_ckv, w_co, w_ffn_gu, w_ffn_down, w_router, b_router,
              w_exp_gu, w_exp_down):
    pos_p = jnp.arange(SEQ, dtype=jnp.int32)
    pos_s = PAST_LEN + jnp.arange(DEC_SEQ, dtype=jnp.int32)
    hp, hs = x_prompt, x_sample
    win_kp, win_vp, win_ks, win_vs = [], [], [], []
    moba_kp, moba_vp, moba_ks, moba_vs = [], [], [], []
    mem_kp, mem_vp = [], []
    for i in range(DEPTH):
        a = i // 2
        qp, kp, vp = _self_qkv(_rms(hp, norm_attn[i]), w_qkv[i], pos_p)
        qs, ks_, vs_ = _self_qkv(_rms(hs, norm_attn[i]), w_qkv[i], pos_s)
        if i % 2 == 0:
            op = _window_prompt(qp, kp, vp, sinks[a])
            os_, nbk, nbv = _window_sample(qs, ks_, vs_, cache_win_k[a], cache_win_v[a], sinks[a], PAST_LEN)
            nbuf = cache_win_k.shape[2]
            win_kp.append(kp[:, -nbuf:])
            win_vp.append(vp[:, -nbuf:])
            win_ks.append(nbk)
            win_vs.append(nbv)
        else:
            op = _moba_prompt(qp, kp, vp)
            os_ = _moba_sample(qs, ks_, vs_, cache_moba_k, cache_moba_v, a, page_table)
            page_shape = (BATCH, SEQ // PAGE_SIZE, PAGE_SIZE, N_KV_HEADS, HEAD_DIM)
            moba_kp.append(kp.reshape(page_shape))
            moba_vp.append(vp.reshape(page_shape))
            moba_ks.append(ks_)
            moba_vs.append(vs_)
        hp = hp + jnp.einsum('nte,ed->ntd', op, w_o[i])
        hs = hs + jnp.einsum('nte,ed->ntd', os_, w_o[i])
        mk, mv = _cross_kv(mem_prompt, w_ckv[i])
        mem_kp.append(mk)
        mem_vp.append(mv)
        hp = hp + _cross_attn(_rms(hp, norm_cross[i]), mk, mv, w_cq[i], w_co[i])
        hs = hs + _cross_attn(_rms(hs, norm_cross[i]), cache_mem_k[i], cache_mem_v[i], w_cq[i], w_co[i])
        xp_n = _rms(hp, norm_ffn[i])
        xs_n = _rms(hs, norm_ffn[i])
        if i % 2 == 0:
            hp = hp + _swiglu(xp_n, w_ffn_gu[a], w_ffn_down[a])
            hs = hs + _swiglu(xs_n, w_ffn_gu[a], w_ffn_down[a])
        else:
            hp = hp + _moe(xp_n, w_router[a], b_router[a], w_exp_gu[a], w_exp_down[a])
            hs = hs + _moe(xs_n, w_router[a], b_router[a], w_exp_gu[a], w_exp_down[a])
    y_prompt = _rms(hp, norm_final)
    y_sample = _rms(hs, norm_final)
    return (y_prompt, y_sample,
            jnp.stack(win_kp), jnp.stack(win_vp), jnp.stack(win_ks), jnp.stack(win_vs),
            jnp.stack(moba_kp), jnp.stack(moba_vp), jnp.stack(moba_ks), jnp.stack(moba_vs),
            jnp.stack(mem_kp), jnp.stack(mem_vp))
```

```python
import functools

import numpy as np
import jax
import jax.numpy as jnp
from jax import lax
from jax.experimental import pallas as pl
from jax.experimental.pallas import tpu as pltpu

D_MODEL = 1024
N_HEADS = 16
N_KV_HEADS = 4
HEAD_DIM = 64
GROUP = N_HEADS // N_KV_HEADS
ATTN_SCALE = HEAD_DIM ** -0.5
WINDOW = 128
MOBA_BLOCK = 256
MOBA_TOPK = 3
PAGE_SIZE = 128
CROSS_HEADS = 4
CROSS_HEAD_DIM = D_MODEL // CROSS_HEADS
CROSS_SCALE = CROSS_HEAD_DIM ** -0.5
EXPERT_TOPK = 2
ROPE_THETA = 10000.0
RMS_EPS = 1e-5

LANES = 128
KV_WIDTH = N_KV_HEADS * HEAD_DIM
Q_WIDTH = N_HEADS * HEAD_DIM
VMEM_LIMIT = 56 * 1024 * 1024
NEG_INF = float("-inf")

F32 = jnp.float32
BF16 = jnp.bfloat16


def _params(*sem):
    return pltpu.CompilerParams(dimension_semantics=sem, vmem_limit_bytes=VMEM_LIMIT)


def _rms(x, g):
    xf = x.astype(F32)
    return xf * lax.rsqrt(jnp.mean(xf * xf, axis=-1, keepdims=True) + RMS_EPS) * g


def _dot(a, b):
    return jnp.dot(a, b, preferred_element_type=F32)


def _dot_nt(a, b, precision=None):
    return lax.dot_general(a, b, (((1,), (1,)), ((), ())), preferred_element_type=F32, precision=precision)


def _lane_half():
    return lax.broadcasted_iota(jnp.int32, (1, LANES), 1) // HEAD_DIM


def _dup_head(x, half):
    return jnp.where(_lane_half() == half, x, pltpu.roll(x, HEAD_DIM, 1))


def _proj_kernel(*refs, norm, res):
    x_ref, w_ref = refs[0], refs[1]
    i = 2
    g_ref = r_ref = None
    if norm:
        g_ref = refs[i]
        i += 1
    if res:
        r_ref = refs[i]
        i += 1
    o_ref = refs[i]
    x = x_ref[...]
    if norm:
        x = _rms(x, g_ref[...])
    y = _dot(x.astype(BF16), w_ref[...])
    if res:
        y = y + r_ref[...]
    o_ref[...] = y.astype(o_ref.dtype)


def _row_tile(t, cap=512):
    for tm in (1024, 512, 256, 128, 64, 32, 16, 8):
        if tm <= cap and t % tm == 0:
            return tm
    raise ValueError(f"row count {t} is not a multiple of 8")


def _proj(x, w, gain=None, res=None, out_dtype=F32):
    t, k = x.shape
    n = w.shape[1]
    tm = _row_tile(t)
    args = [x, w]
    specs = [pl.BlockSpec((tm, k), lambda i: (i, 0)), pl.BlockSpec((k, n), lambda i: (0, 0))]
    if gain is not None:
        args.append(gain.reshape(1, k))
        specs.append(pl.BlockSpec((1, k), lambda i: (0, 0)))
    if res is not None:
        args.append(res)
        specs.append(pl.BlockSpec((tm, n), lambda i: (i, 0)))
    return pl.pallas_call(
        functools.partial(_proj_kernel, norm=gain is not None, res=res is not None),
        grid=(t // tm,),
        in_specs=specs,
        out_specs=pl.BlockSpec((tm, n), lambda i: (i, 0)),
        out_shape=jax.ShapeDtypeStruct((t, n), out_dtype),
        compiler_params=_params("parallel"),
    )(*args)


def _qkv_kernel(x_ref, g_ref, w_ref, cos_ref, sin_ref, q_ref, k_ref, v_ref):
    xn = _rms(x_ref[...], g_ref[...]).astype(BF16)
    y = _dot(xn, w_ref[...])
    cos = cos_ref[...]
    sin = sin_ref[...]
    first = lax.broadcasted_iota(jnp.int32, (1, LANES), 1) % HEAD_DIM < HEAD_DIM // 2
    n_rot = (Q_WIDTH + KV_WIDTH) // LANES
    for c in range(n_rot):
        yc = y[:, c * LANES:(c + 1) * LANES]
        rot = jnp.where(first, pltpu.roll(yc, LANES - HEAD_DIM // 2, 1), pltpu.roll(yc, HEAD_DIM // 2, 1))
        r = yc * cos + rot * sin
        if c < Q_WIDTH // LANES:
            q_ref[:, c * LANES:(c + 1) * LANES] = r * ATTN_SCALE
        else:
            c0 = c - Q_WIDTH // LANES
            k_ref[:, c0 * LANES:(c0 + 1) * LANES] = r
    v_ref[...] = y[:, Q_WIDTH + KV_WIDTH:]


def _qkv(h, gain, w, cos, sin):
    t = h.shape[0]
    tm = _row_tile(t)
    wq = w.shape[1]
    return pl.pallas_call(
        _qkv_kernel,
        grid=(t // tm,),
        in_specs=[pl.BlockSpec((tm, D_MODEL), lambda i: (i, 0)),
                  pl.BlockSpec((1, D_MODEL), lambda i: (0, 0)),
                  pl.BlockSpec((D_MODEL, wq), lambda i: (0, 0)),
                  pl.BlockSpec((tm, LANES), lambda i: (i, 0)),
                  pl.BlockSpec((tm, LANES), lambda i: (i, 0))],
        out_specs=[pl.BlockSpec((tm, Q_WIDTH), lambda i: (i, 0)),
                   pl.BlockSpec((tm, KV_WIDTH), lambda i: (i, 0)),
                   pl.BlockSpec((tm, KV_WIDTH), lambda i: (i, 0))],
        out_shape=[jax.ShapeDtypeStruct((t, Q_WIDTH), F32),
                   jax.ShapeDtypeStruct((t, KV_WIDTH), F32),
                   jax.ShapeDtypeStruct((t, KV_WIDTH), F32)],
        compiler_params=_params("parallel"),
    )(h, gain.reshape(1, D_MODEL), w, cos, sin)


def _rope_tables(pos):
    half = HEAD_DIM // 2
    inv_freq = ROPE_THETA ** (-jnp.arange(half, dtype=F32) / half)
    ang = pos.astype(F32)[:, None] * inv_freq[None, :]
    cos = jnp.cos(ang)
    sin = jnp.sin(ang)
    reps = LANES // HEAD_DIM
    return (jnp.tile(jnp.concatenate([cos, cos], axis=-1), (1, reps)),
            jnp.tile(jnp.concatenate([-sin, sin], axis=-1), (1, reps)))


def _stack_group(q_chunks, rows):
    half = _lane_half()
    parts = []
    for g in range(GROUP):
        qc = q_chunks[g // 2]
        parts.append(jnp.where(half == g % 2, qc, jnp.zeros_like(qc)))
    return jnp.concatenate(parts, axis=0)


def _unstack_group(o4, rows):
    low = _lane_half() == 0
    return [jnp.where(low, o4[(2 * p) * rows:(2 * p + 1) * rows], o4[(2 * p + 1) * rows:(2 * p + 2) * rows])
            for p in range(2)]


def _sink_column(sink_ref, kv, rows):
    return jnp.concatenate([jnp.full((rows, 1), sink_ref[kv * GROUP + g], F32) for g in range(GROUP)], axis=0)


def _sink_softmax_pv(s, sink, vd):
    m = jnp.maximum(jnp.max(s, axis=-1, keepdims=True), sink)
    e = jnp.exp(s - m)
    den = jnp.sum(e, axis=-1, keepdims=True) + jnp.exp(sink - m)
    return _dot(e.astype(BF16), vd) / den


def _win_prompt_kernel(sink_ref, q_ref, kc_ref, kp_ref, vc_ref, vp_ref, o_ref):
    i = pl.program_id(1)
    w = WINDOW
    qi = lax.broadcasted_iota(jnp.int32, (w, 2 * w), 0)
    li = lax.broadcasted_iota(jnp.int32, (w, 2 * w), 1)
    ok = (li > qi) & (li <= qi + w) & (li >= jnp.where(i > 0, 0, w))
    ok4 = jnp.concatenate([ok] * GROUP, axis=0)
    for kv in range(N_KV_HEADS):
        cs = slice((kv // 2) * LANES, (kv // 2 + 1) * LANES)
        kd = _dup_head(jnp.concatenate([kp_ref[:, cs], kc_ref[:, cs]], axis=0), kv % 2).astype(BF16)
        vd = _dup_head(jnp.concatenate([vp_ref[:, cs], vc_ref[:, cs]], axis=0), kv % 2).astype(BF16)
        q4 = _stack_group([q_ref[:, (2 * kv + p) * LANES:(2 * kv + p + 1) * LANES] for p in range(2)], w)
        s = jnp.where(ok4, _dot_nt(q4.astype(BF16), kd), NEG_INF)
        o4 = _sink_softmax_pv(s, _sink_column(sink_ref, kv, w), vd)
        for p, oc in enumerate(_unstack_group(o4, w)):
            o_ref[:, (2 * kv + p) * LANES:(2 * kv + p + 1) * LANES] = oc.astype(o_ref.dtype)


def _win_prompt(q, k, v, sinks, batch, seq):
    nb = seq // WINDOW
    cur = lambda b, i: (b * nb + i, 0)
    prev = lambda b, i: (b * nb + jnp.maximum(i - 1, 0), 0)
    return pl.pallas_call(
        _win_prompt_kernel,
        grid=(batch, nb),
        in_specs=[pl.BlockSpec(memory_space=pltpu.SMEM),
                  pl.BlockSpec((WINDOW, Q_WIDTH), cur),
                  pl.BlockSpec((WINDOW, KV_WIDTH), cur), pl.BlockSpec((WINDOW, KV_WIDTH), prev),
                  pl.BlockSpec((WINDOW, KV_WIDTH), cur), pl.BlockSpec((WINDOW, KV_WIDTH), prev)],
        out_specs=pl.BlockSpec((WINDOW, Q_WIDTH), cur),
        out_shape=jax.ShapeDtypeStruct((batch * seq, Q_WIDTH), F32),
        compiler_params=_params("parallel", "parallel"),
    )(sinks, q, k, k, v, v)


def _win_sample_kernel(sink_ref, q_ref, kn_ref, vn_ref, kb_ref, vb_ref, o_ref, kbo_ref, vbo_ref, *, seqs, t):
    nbuf = WINDOW
    nkey = 2 * WINDOW
    ti = lax.broadcasted_iota(jnp.int32, (t, nkey), 0)
    li = lax.broadcasted_iota(jnp.int32, (t, nkey), 1)
    ok = (li < nbuf + t) & (li - nbuf <= ti) & (li - nbuf > ti - WINDOW)
    ok4 = jnp.concatenate([ok] * GROUP, axis=0)
    pad = jnp.zeros((nkey - nbuf - t, KV_WIDTH), F32)
    for s_ in range(seqs):
        rows = slice(s_ * t, (s_ + 1) * t)
        kk = jnp.concatenate([kb_ref[s_], kn_ref[rows, :], pad], axis=0)
        vv = jnp.concatenate([vb_ref[s_], vn_ref[rows, :], pad], axis=0)
        kbo_ref[s_] = kk[t:nbuf + t]
        vbo_ref[s_] = vv[t:nbuf + t]
        for kv in range(N_KV_HEADS):
            cs = slice((kv // 2) * LANES, (kv // 2 + 1) * LANES)
            kd = _dup_head(kk[:, cs], kv % 2).astype(BF16)
            vd = _dup_head(vv[:, cs], kv % 2).astype(BF16)
            q4 = _stack_group([q_ref[rows, (2 * kv + p) * LANES:(2 * kv + p + 1) * LANES] for p in range(2)], t)
            sc = jnp.where(ok4, _dot_nt(q4.astype(BF16), kd), NEG_INF)
            o4 = _sink_softmax_pv(sc, _sink_column(sink_ref, kv, t), vd)
            for p, oc in enumerate(_unstack_group(o4, t)):
                o_ref[rows, (2 * kv + p) * LANES:(2 * kv + p + 1) * LANES] = oc.astype(o_ref.dtype)


def _win_sample(q, k, v, k_buf, v_buf, sinks, row0, n, t):
    seqs = 8 if n % 8 == 0 else 1
    rb = seqs * t
    assert row0 % rb == 0 and k_buf.shape[1] == WINDOW and t % 8 == 0
    new = lambda i: (row0 // rb + i, 0)
    buf = lambda i: (i, 0, 0)
    return pl.pallas_call(
        functools.partial(_win_sample_kernel, seqs=seqs, t=t),
        grid=(n // seqs,),
        in_specs=[pl.BlockSpec(memory_space=pltpu.SMEM),
                  pl.BlockSpec((rb, Q_WIDTH), new),
                  pl.BlockSpec((rb, KV_WIDTH), new), pl.BlockSpec((rb, KV_WIDTH), new),
                  pl.BlockSpec((seqs, WINDOW, KV_WIDTH), buf), pl.BlockSpec((seqs, WINDOW, KV_WIDTH), buf)],
        out_specs=[pl.BlockSpec((rb, Q_WIDTH), lambda i: (i, 0)),
                   pl.BlockSpec((seqs, WINDOW, KV_WIDTH), buf), pl.BlockSpec((seqs, WINDOW, KV_WIDTH), buf)],
        out_shape=[jax.ShapeDtypeStruct((n * t, Q_WIDTH), F32),
                   jax.ShapeDtypeStruct((n, WINDOW, KV_WIDTH), F32),
                   jax.ShapeDtypeStruct((n, WINDOW, KV_WIDTH), F32)],
        compiler_params=_params("parallel"),
    )(sinks, q, k, v, k_buf, v_buf)


def _block_mean_kernel(k_ref, o_ref, *, blocks):
    x = k_ref[...]
    o_ref[...] = jnp.sum(x.reshape(blocks, MOBA_BLOCK, x.shape[-1]), axis=1) * (1.0 / MOBA_BLOCK)


def _block_means(k, rows):
    nblk = rows // MOBA_BLOCK
    per = 8
    assert nblk % per == 0
    return pl.pallas_call(
        functools.partial(_block_mean_kernel, blocks=per),
        grid=(nblk // per,),
        in_specs=[pl.BlockSpec((per * MOBA_BLOCK, KV_WIDTH), lambda i: (i, 0))],
        out_specs=pl.BlockSpec((per, KV_WIDTH), lambda i: (i, 0)),
        out_shape=jax.ShapeDtypeStruct((nblk, KV_WIDTH), F32),
        compiler_params=_params("parallel"),
    )(k)


def _top_blocks(gate, valid, topk):
    cols = float(gate.shape[1])
    col = lax.broadcasted_iota(jnp.int32, gate.shape, 1).astype(F32)
    g = jnp.where(valid, gate, NEG_INF)
    sel = jnp.zeros(gate.shape, F32)
    for _ in range(topk):
        mx = jnp.max(g, axis=1, keepdims=True)
        idx = jnp.min(jnp.where(g == mx, col, cols), axis=1, keepdims=True)
        pick = col == idx
        sel = jnp.where(pick, 1.0, sel)
        g = jnp.where(pick, NEG_INF, g)
    return jnp.where(valid, sel, 0.0)


def _moba_prompt_kernel(qi_ref, kj_ref, q_ref, k_ref, v_ref, km_ref, o_ref,
                        q4_ref, sel_ref, m_ref, l_ref, acc_ref, *, nb):
    kv = pl.program_id(1)
    step = pl.program_id(2)
    i = qi_ref[step]
    j = kj_ref[step]
    blk = MOBA_BLOCK
    half = kv % 2

    @pl.when(j == 0)
    def _():
        q4 = _stack_group([q_ref[:, p * LANES:(p + 1) * LANES] for p in range(2)], blk)
        kmd = _dup_head(km_ref[...], half)
        gate = _dot_nt(q4, kmd, precision=lax.Precision.HIGHEST)
        col = lax.broadcasted_iota(jnp.int32, gate.shape, 1)
        sel_ref[...] = _top_blocks(gate, col < i, MOBA_TOPK)
        q4_ref[...] = q4.astype(BF16)
        m_ref[...] = jnp.full(m_ref.shape, NEG_INF, F32)
        l_ref[...] = jnp.zeros(l_ref.shape, F32)
        acc_ref[...] = jnp.zeros(acc_ref.shape, F32)

    kd = _dup_head(k_ref[...], half).astype(BF16)
    vd = _dup_head(v_ref[...], half).astype(BF16)
    s = _dot_nt(q4_ref[...], kd)
    sel = sel_ref[...]
    col = lax.broadcasted_iota(jnp.int32, sel.shape, 1)
    picked = jnp.sum(jnp.where(col == j, sel, 0.0), axis=1, keepdims=True)
    qpos = (lax.broadcasted_iota(jnp.int32, (GROUP * blk, 1), 0) % blk).astype(F32)
    kpos = lax.broadcasted_iota(jnp.int32, (GROUP * blk, blk), 1).astype(F32)
    own = jnp.where(j == i, 1.0, 0.0)
    limit = own * qpos + (1.0 - own) * (picked * (blk + 1.0) - 1.0)
    s = jnp.where(kpos <= limit, s, NEG_INF)
    m_old = m_ref[...]
    m_new = jnp.maximum(m_old, jnp.max(s, axis=1, keepdims=True))
    m_use = jnp.where(m_new == NEG_INF, 0.0, m_new)
    alpha = jnp.exp(m_old - m_use)
    p = jnp.exp(s - m_use)
    l_ref[...] = alpha * l_ref[...] + jnp.sum(p, axis=1, keepdims=True)
    acc_ref[...] = alpha * acc_ref[...] + _dot(p.astype(BF16), vd)
    m_ref[...] = m_new

    @pl.when(j == i)
    def _():
        o4 = acc_ref[...] / l_ref[...]
        for p_, oc in enumerate(_unstack_group(o4, blk)):
            o_ref[:, p_ * LANES:(p_ + 1) * LANES] = oc.astype(o_ref.dtype)


def _moba_prompt(q, k, v, batch, seq):
    assert seq % MOBA_BLOCK == 0
    nb = seq // MOBA_BLOCK
    kmean = _block_means(k, batch * seq)
    pairs = [(i, j) for i in range(nb) for j in range(i + 1)]
    qi = jnp.asarray(np.array([p[0] for p in pairs], np.int32))
    kj = jnp.asarray(np.array([p[1] for p in pairs], np.int32))
    blk = MOBA_BLOCK
    grid_spec = pltpu.PrefetchScalarGridSpec(
        num_scalar_prefetch=2,
        grid=(batch, N_KV_HEADS, len(pairs)),
        in_specs=[pl.BlockSpec((blk, 2 * LANES), lambda b, h, s, qi, kj: (b * nb + qi[s], h)),
                  pl.BlockSpec((blk, LANES), lambda b, h, s, qi, kj: (b * nb + kj[s], h // 2)),
                  pl.BlockSpec((blk, LANES), lambda b, h, s, qi, kj: (b * nb + kj[s], h // 2)),
                  pl.BlockSpec((nb, LANES), lambda b, h, s, qi, kj: (b, h // 2))],
        out_specs=pl.BlockSpec((blk, 2 * LANES), lambda b, h, s, qi, kj: (b * nb + qi[s], h)),
        scratch_shapes=[pltpu.VMEM((GROUP * blk, LANES), BF16),
                        pltpu.VMEM((GROUP * blk, nb), F32),
                        pltpu.VMEM((GROUP * blk, 1), F32),
                        pltpu.VMEM((GROUP * blk, 1), F32),
                        pltpu.VMEM((GROUP * blk, LANES), F32)])
    return pl.pallas_call(
        functools.partial(_moba_prompt_kernel, nb=nb),
        grid_spec=grid_spec,
        out_shape=jax.ShapeDtypeStruct((batch * seq, Q_WIDTH), F32),
        compiler_params=_params("parallel", "parallel", "arbitrary"),
    )(qi, kj, q, k, v, kmean)


def _moba_sample_kernel(pt_ref, q_ref, kn_ref, vn_ref, *refs, pages, n_pages, t, past):
    k_refs = refs[:pages]
    v_refs = refs[pages:2 * pages]
    o_ref = refs[2 * pages]
    qbd_ref, s_ref, kmt_ref, bmax_ref, selp_ref, m_ref, lacc_ref, acc_ref, snew_ref = refs[2 * pages + 1:]
    c = pl.program_id(1)
    chunks = n_pages // pages
    ppb = MOBA_BLOCK // PAGE_SIZE
    n_blk = n_pages // ppb
    rows = N_HEADS * t
    lane = lax.broadcasted_iota(jnp.int32, (1, LANES), 1)
    lane_half = _lane_half()

    @pl.when(c == 0)
    def _():
        parts = []
        for h in range(N_HEADS):
            kv = h // GROUP
            src = q_ref[:, (h // 2) * LANES:(h // 2 + 1) * LANES]
            if h % 2 != kv % 2:
                src = pltpu.roll(src, HEAD_DIM, 1)
            src = jnp.where(lane_half == kv % 2, src, jnp.zeros_like(src))
            zero = jnp.zeros_like(src)
            parts.append(jnp.concatenate([src, zero] if kv // 2 == 0 else [zero, src], axis=1))
        qbd_ref[...] = jnp.concatenate(parts, axis=0)
        kmt_ref[...] = jnp.zeros(kmt_ref.shape, F32)
        bmax_ref[...] = jnp.full(bmax_ref.shape, NEG_INF, F32)

    @pl.when(c < chunks)
    def _():
        qbd = qbd_ref[...].astype(BF16)
        for b in range(pages // ppb):
            ksum = None
            for pp in range(ppb):
                p_ = b * ppb + pp
                kt = k_refs[p_][0]
                ksum = kt if ksum is None else ksum + kt
                s = _dot(qbd, kt.astype(BF16))
                page = c * pages + p_
                s_ref[page] = s
                bmax_ref[...] = jnp.where(lane == page, jnp.max(s, axis=1, keepdims=True), bmax_ref[...])
            blk = c * (pages // ppb) + b
            kmt_ref[...] = jnp.where(lane == blk, jnp.sum(ksum, axis=1, keepdims=True) * (1.0 / MOBA_BLOCK),
                                     kmt_ref[...])

    @pl.when(c == chunks - 1)
    def _():
        gate = jnp.dot(qbd_ref[...], kmt_ref[...], preferred_element_type=F32, precision=lax.Precision.HIGHEST)
        own = (past + lax.broadcasted_iota(jnp.int32, (rows, LANES), 0) % t) // MOBA_BLOCK
        col = lax.broadcasted_iota(jnp.int32, (rows, LANES), 1)
        sel = _top_blocks(gate, (col < own) & (col < n_blk), MOBA_TOPK)
        expand = (lax.broadcasted_iota(jnp.int32, (LANES, LANES), 0)
                  == lax.broadcasted_iota(jnp.int32, (LANES, LANES), 1) // ppb)
        selp = _dot(sel.astype(BF16), expand.astype(BF16))
        selp_ref[...] = selp
        pad = jnp.zeros((LANES - t, KV_WIDTH), F32)
        knp = jnp.concatenate([kn_ref[...], pad], axis=0).astype(BF16)
        s_new = _dot_nt(qbd_ref[...].astype(BF16), knp)
        tq = lax.broadcasted_iota(jnp.int32, (rows, LANES), 0) % t
        own_start = ((past + tq) // MOBA_BLOCK) * MOBA_BLOCK
        ok_new = (col < t) & (past + col >= own_start) & (col <= tq)
        s_new = jnp.where(ok_new, s_new, NEG_INF)
        snew_ref[...] = s_new
        m = jnp.maximum(jnp.max(jnp.where(selp > 0.0, bmax_ref[...], NEG_INF), axis=1, keepdims=True),
                        jnp.max(s_new, axis=1, keepdims=True))
        tail_pages = min(ppb, n_pages)
        for p_ in range(n_pages - tail_pages, n_pages):
            kpos = p_ * PAGE_SIZE + col
            ok_t = (kpos >= own_start) & (kpos <= past + tq)
            m = jnp.maximum(m, jnp.max(jnp.where(ok_t, s_ref[p_], NEG_INF), axis=1, keepdims=True))
        m_ref[...] = m
        lacc_ref[...] = jnp.zeros(lacc_ref.shape, F32)
        acc_ref[...] = jnp.zeros(acc_ref.shape, F32)

    @pl.when(c >= chunks)
    def _():
        m = m_ref[...]
        selp = selp_ref[...]
        tq = lax.broadcasted_iota(jnp.int32, (rows, LANES), 0) % t
        col = lax.broadcasted_iota(jnp.int32, (rows, LANES), 1)
        own_start = ((past + tq) // MOBA_BLOCK) * MOBA_BLOCK
        for p_ in range(pages):
            page = (c - chunks) * pages + p_
            picked = jnp.sum(jnp.where(lane == page, selp, 0.0), axis=1, keepdims=True)
            kpos = page * PAGE_SIZE + col
            in_own = (kpos >= own_start) & (kpos <= past + tq)
            keep = picked + jnp.where(in_own, 1.0, 0.0)
            p = jnp.exp(jnp.where(keep > 0.0, s_ref[page] - m, NEG_INF))
            lacc_ref[...] += p
            acc_ref[...] += _dot_nt(p.astype(BF16), v_refs[p_][0].astype(BF16))

    @pl.when(c == 2 * chunks - 1)
    def _():
        m = m_ref[...]
        p_new = jnp.exp(snew_ref[...] - m)
        pad = jnp.zeros((LANES - t, KV_WIDTH), F32)
        vnp = jnp.concatenate([vn_ref[...], pad], axis=0).astype(BF16)
        acc = acc_ref[...] + _dot(p_new.astype(BF16), vnp)
        den = jnp.sum(lacc_ref[...] + p_new, axis=1, keepdims=True)
        o = acc / den
        for j in range(N_HEADS // 2):
            kv = j // 2
            cs = slice((kv // 2) * LANES, (kv // 2 + 1) * LANES)
            a = o[(2 * j) * t:(2 * j + 1) * t, cs]
            b = o[(2 * j + 1) * t:(2 * j + 2) * t, cs]
            if kv % 2 == 0:
                oc = jnp.where(lane_half == 0, a, pltpu.roll(b, HEAD_DIM, 1))
            else:
                oc = jnp.where(lane_half == 0, pltpu.roll(a, HEAD_DIM, 1), b)
            o_ref[:, j * LANES:(j + 1) * LANES] = oc.astype(o_ref.dtype)


def _moba_sample(q, k, v, kpool_t, vpool_t, page_table, pool_base, row0, n, t):
    n_pages = page_table.shape[1]
    past = n_pages * PAGE_SIZE
    ppb = MOBA_BLOCK // PAGE_SIZE
    pages = min(16, n_pages)
    assert n_pages % pages == 0 and pages % ppb == 0 and n_pages <= LANES and row0 % t == 0 and t == 8
    assert past % MOBA_BLOCK == 0
    chunks = n_pages // pages
    rows = N_HEADS * t
    new = lambda i, c, pt: (row0 // t + i, 0)

    def k_map(p_):
        return lambda i, c, pt: (pool_base + pt[i, jnp.minimum(c, chunks - 1) * pages + p_], 0, 0)

    def v_map(p_):
        return lambda i, c, pt: (pool_base + pt[i, jnp.maximum(c - chunks, 0) * pages + p_], 0, 0)

    page_block = (1, KV_WIDTH, PAGE_SIZE)
    grid_spec = pltpu.PrefetchScalarGridSpec(
        num_scalar_prefetch=1,
        grid=(n, 2 * chunks),
        in_specs=([pl.BlockSpec((t, Q_WIDTH), new), pl.BlockSpec((t, KV_WIDTH), new), pl.BlockSpec((t, KV_WIDTH), new)]
                  + [pl.BlockSpec(page_block, k_map(p_)) for p_ in range(pages)]
                  + [pl.BlockSpec(page_block, v_map(p_)) for p_ in range(pages)]),
        out_specs=pl.BlockSpec((t, Q_WIDTH), lambda i, c, pt: (i, 0)),
        scratch_shapes=[pltpu.VMEM((rows, KV_WIDTH), F32),
                        pltpu.VMEM((n_pages, rows, PAGE_SIZE), F32),
                        pltpu.VMEM((KV_WIDTH, LANES), F32),
                        pltpu.VMEM((rows, LANES), F32),
                        pltpu.VMEM((rows, LANES), F32),
                        pltpu.VMEM((rows, 1), F32),
                        pltpu.VMEM((rows, LANES), F32),
                        pltpu.VMEM((rows, KV_WIDTH), F32),
                        pltpu.VMEM((rows, LANES), F32)])
    return pl.pallas_call(
        functools.partial(_moba_sample_kernel, pages=pages, n_pages=n_pages, t=t, past=past),
        grid_spec=grid_spec,
        out_shape=jax.ShapeDtypeStruct((n * t, Q_WIDTH), F32),
        compiler_params=_params("parallel", "arbitrary"),
    )(page_table, q, k, v, *([kpool_t] * pages), *([vpool_t] * pages))


def _cross_kernel(q_ref, mk_ref, mv_ref, o_ref):
    q = q_ref[...].astype(BF16)
    for h in range(CROSS_HEADS):
        cs = slice(h * CROSS_HEAD_DIM, (h + 1) * CROSS_HEAD_DIM)
        s = _dot_nt(q[:, cs], mk_ref[0, :, cs].astype(BF16)) * CROSS_SCALE
        m = jnp.max(s, axis=-1, keepdims=True)
        e = jnp.exp(s - m)
        o = _dot(e.astype(BF16), mv_ref[0, :, cs].astype(BF16)) / jnp.sum(e, axis=-1, keepdims=True)
        o_ref[:, cs] = o.astype(o_ref.dtype)


def _cross_attend(qc, mk, mv, row0, nseq, rows_per_seq):
    tq = _row_tile(rows_per_seq)
    per = rows_per_seq // tq
    assert row0 % tq == 0
    mem = mk.shape[1]
    return pl.pallas_call(
        _cross_kernel,
        grid=(nseq, per),
        in_specs=[pl.BlockSpec((tq, D_MODEL), lambda s, r: (row0 // tq + s * per + r, 0)),
                  pl.BlockSpec((1, mem, D_MODEL), lambda s, r: (s, 0, 0)),
                  pl.BlockSpec((1, mem, D_MODEL), lambda s, r: (s, 0, 0))],
        out_specs=pl.BlockSpec((tq, D_MODEL), lambda s, r: (s * per + r, 0)),
        out_shape=jax.ShapeDtypeStruct((nseq * rows_per_seq, D_MODEL), F32),
        compiler_params=_params("parallel", "parallel"),
    )(qc, mk, mv)


def _silu(x):
    return x * (1.0 / (1.0 + jnp.exp(-x)))


def _ffn_kernel(x_ref, g_ref, wg_ref, wu_ref, wd_ref, o_ref, xn_ref):
    f = pl.program_id(1)

    @pl.when(f == 0)
    def _():
        x = x_ref[...]
        xn_ref[...] = _rms(x, g_ref[...]).astype(BF16)
        o_ref[...] = x

    xn = xn_ref[...]
    hid = _silu(_dot(xn, wg_ref[...])) * _dot(xn, wu_ref[...])
    o_ref[...] += _dot(hid.astype(BF16), wd_ref[...])


def _col_tile(f, cap):
    best = None
    for tf in range(LANES, cap + 1, LANES):
        if f % tf == 0:
            best = tf
    assert best is not None
    return best


def _ffn(h, gain, w_gu, w_down):
    t = h.shape[0]
    f = w_down.shape[0]
    tm = _row_tile(t)
    tf = _col_tile(f, 1536)
    nf = f // tf
    return pl.pallas_call(
        _ffn_kernel,
        grid=(t // tm, nf),
        in_specs=[pl.BlockSpec((tm, D_MODEL), lambda i, j: (i, 0)),
                  pl.BlockSpec((1, D_MODEL), lambda i, j: (0, 0)),
                  pl.BlockSpec((D_MODEL, tf), lambda i, j: (0, j)),
                  pl.BlockSpec((D_MODEL, tf), lambda i, j: (0, nf + j)),
                  pl.BlockSpec((tf, D_MODEL), lambda i, j: (j, 0))],
        out_specs=pl.BlockSpec((tm, D_MODEL), lambda i, j: (i, 0)),
        out_shape=jax.ShapeDtypeStruct((t, D_MODEL), F32),
        scratch_shapes=[pltpu.VMEM((tm, D_MODEL), BF16)],
        compiler_params=_params("parallel", "arbitrary"),
    )(h, gain.reshape(1, D_MODEL), w_gu, w_gu, w_down)


def _router_kernel(x_ref, g_ref, w_ref, b_ref, o_ref):
    xn = _rms(x_ref[...], g_ref[...])
    logits = jnp.dot(xn, w_ref[...], preferred_element_type=F32, precision=lax.Precision.HIGHEST) + b_ref[...]
    n_exp = logits.shape[1]
    col = lax.broadcasted_iota(jnp.int32, logits.shape, 1)
    v1 = jnp.max(logits, axis=1, keepdims=True)
    i1 = jnp.min(jnp.where(logits == v1, col, n_exp), axis=1, keepdims=True)
    rest = jnp.where(col == i1, NEG_INF, logits)
    v2 = jnp.max(rest, axis=1, keepdims=True)
    i2 = jnp.min(jnp.where(rest == v2, col, n_exp), axis=1, keepdims=True)
    e2 = jnp.exp(v2 - v1)
    den = 1.0 + e2
    o_ref[...] = jnp.where(col == i1, 1.0 / den, 0.0) + jnp.where(col == i2, e2 / den, 0.0)


def _router(h, gain, w_router, b_router):
    t = h.shape[0]
    n_exp = w_router.shape[1]
    tm = _row_tile(t)
    return pl.pallas_call(
        _router_kernel,
        grid=(t // tm,),
        in_specs=[pl.BlockSpec((tm, D_MODEL), lambda i: (i, 0)),
                  pl.BlockSpec((1, D_MODEL), lambda i: (0, 0)),
                  pl.BlockSpec((D_MODEL, n_exp), lambda i: (0, 0)),
                  pl.BlockSpec((1, n_exp), lambda i: (0, 0))],
        out_specs=pl.BlockSpec((tm, n_exp), lambda i: (i, 0)),
        out_shape=jax.ShapeDtypeStruct((t, n_exp), F32),
        compiler_params=_params("parallel"),
    )(h, gain.reshape(1, D_MODEL), w_router, b_router.reshape(1, n_exp))


def _moe_kernel(x_ref, g_ref, cw_ref, wg_ref, wu_ref, wd_ref, o_ref, xn_ref):
    e = pl.program_id(1)
    f = pl.program_id(2)

    @pl.when((e == 0) & (f == 0))
    def _():
        x = x_ref[...]
        xn_ref[...] = _rms(x, g_ref[...]).astype(BF16)
        o_ref[...] = x

    cw = cw_ref[...]
    col = lax.broadcasted_iota(jnp.int32, cw.shape, 1)
    w = jnp.sum(jnp.where(col == e, cw, 0.0), axis=1, keepdims=True)
    xn = xn_ref[...]
    hid = _silu(_dot(xn, wg_ref[0])) * _dot(xn, wu_ref[0]) * w
    o_ref[...] += _dot(hid.astype(BF16), wd_ref[0])


def _moe(h, gain, combine, w_gu, w_down):
    t = h.shape[0]
    n_exp, f, _ = w_down.shape
    tm = _row_tile(t)
    tf = _col_tile(f, 1024)
    nf = f // tf
    return pl.pallas_call(
        _moe_kernel,
        grid=(t // tm, n_exp, nf),
        in_specs=[pl.BlockSpec((tm, D_MODEL), lambda i, e, j: (i, 0)),
                  pl.BlockSpec((1, D_MODEL), lambda i, e, j: (0, 0)),
                  pl.BlockSpec((tm, n_exp), lambda i, e, j: (i, 0)),
                  pl.BlockSpec((1, D_MODEL, tf), lambda i, e, j: (e, 0, j)),
                  pl.BlockSpec((1, D_MODEL, tf), lambda i, e, j: (e, 0, nf + j)),
                  pl.BlockSpec((1, tf, D_MODEL), lambda i, e, j: (e, j, 0))],
        out_specs=pl.BlockSpec((tm, D_MODEL), lambda i, e, j: (i, 0)),
        out_shape=jax.ShapeDtypeStruct((t, D_MODEL), F32),
        scratch_shapes=[pltpu.VMEM((tm, D_MODEL), BF16)],
        compiler_params=_params("parallel", "arbitrary", "arbitrary"),
    )(h, gain.reshape(1, D_MODEL), combine, w_gu, w_gu, w_down)


def _final_norm_kernel(x_ref, g_ref, o_ref):
    o_ref[...] = _rms(x_ref[...], g_ref[...])


def _final_norm(h, gain, row0, rows):
    tm = _row_tile(rows)
    assert row0 % tm == 0
    return pl.pallas_call(
        _final_norm_kernel,
        grid=(rows // tm,),
        in_specs=[pl.BlockSpec((tm, D_MODEL), lambda i: (row0 // tm + i, 0)),
                  pl.BlockSpec((1, D_MODEL), lambda i: (0, 0))],
        out_specs=pl.BlockSpec((tm, D_MODEL), lambda i: (i, 0)),
        out_shape=jax.ShapeDtypeStruct((rows, D_MODEL), F32),
        compiler_params=_params("parallel"),
    )(h, gain.reshape(1, D_MODEL))


def _key_minor(pool):
    l, p, page, kvh, d = pool.shape
    return pool.transpose(0, 1, 3, 4, 2).reshape(l * p, kvh * d, page)


def kernel(x_prompt, x_sample, cache_win_k, cache_win_v, cache_moba_k, cache_moba_v, page_table, cache_mem_k, cache_mem_v, mem_prompt, norm_attn, norm_cross, norm_ffn, norm_final, w_qkv, w_o, sinks, w_cq, w_ckv, w_co, w_ffn_gu, w_ffn_down, w_router, b_router, w_exp_gu, w_exp_down):
    batch, seq, d = x_prompt.shape
    n, t, _ = x_sample.shape
    depth = norm_attn.shape[0]
    tp, ts = batch * seq, n * t
    n_pages = page_table.shape[1]
    past = n_pages * PAGE_SIZE
    n_phys = cache_moba_k.shape[1]
    mem_len = mem_prompt.shape[1]
    assert d == D_MODEL and seq % MOBA_BLOCK == 0 and seq >= WINDOW and cache_win_k.shape[2] == WINDOW

    h = jnp.concatenate([x_prompt.reshape(tp, d), x_sample.reshape(ts, d)], axis=0)
    pos = jnp.concatenate([jnp.tile(jnp.arange(seq, dtype=jnp.int32), batch),
                           jnp.tile(past + jnp.arange(t, dtype=jnp.int32), n)])
    cos, sin = _rope_tables(pos)
    kpool_t = _key_minor(cache_moba_k)
    vpool_t = _key_minor(cache_moba_v)
    mem_rows = mem_prompt.reshape(batch * mem_len, d)

    win_kp, win_vp, win_ks, win_vs = [], [], [], []
    moba_kp, moba_vp, moba_ks, moba_vs = [], [], [], []
    mem_kp, mem_vp = [], []
    for i in range(depth):
        a = i // 2
        q, k, v = _qkv(h, norm_attn[i], w_qkv[i].astype(BF16), cos, sin)
        kp4 = k[:tp].reshape(batch, seq, N_KV_HEADS, HEAD_DIM)
        vp4 = v[:tp].reshape(batch, seq, N_KV_HEADS, HEAD_DIM)
        if i % 2 == 0:
            op = _win_prompt(q, k, v, sinks[a], batch, seq)
            os_, nbk, nbv = _win_sample(q, k, v, cache_win_k[a].reshape(n, WINDOW, KV_WIDTH),
                                        cache_win_v[a].reshape(n, WINDOW, KV_WIDTH), sinks[a], tp, n, t)
            win_kp.append(kp4[:, seq - WINDOW:])
            win_vp.append(vp4[:, seq - WINDOW:])
            win_ks.append(nbk.reshape(n, WINDOW, N_KV_HEADS, HEAD_DIM))
            win_vs.append(nbv.reshape(n, WINDOW, N_KV_HEADS, HEAD_DIM))
        else:
            op = _moba_prompt(q, k, v, batch, seq)
            os_ = _moba_sample(q, k, v, kpool_t, vpool_t, page_table, a * n_phys, tp, n, t)
            moba_kp.append(kp4.reshape(batch, seq // PAGE_SIZE, PAGE_SIZE, N_KV_HEADS, HEAD_DIM))
            moba_vp.append(vp4.reshape(batch, seq // PAGE_SIZE, PAGE_SIZE, N_KV_HEADS, HEAD_DIM))
            moba_ks.append(k[tp:].reshape(n, t, N_KV_HEADS, HEAD_DIM))
            moba_vs.append(v[tp:].reshape(n, t, N_KV_HEADS, HEAD_DIM))
        h = _proj(jnp.concatenate([op, os_], axis=0), w_o[i].astype(BF16), res=h)

        mkv = _proj(mem_rows, w_ckv[i].astype(BF16))
        mk = mkv[:, :d].reshape(batch, mem_len, d)
        mv = mkv[:, d:].reshape(batch, mem_len, d)
        mem_kp.append(mk.reshape(batch, mem_len, CROSS_HEADS, CROSS_HEAD_DIM))
        mem_vp.append(mv.reshape(batch, mem_len, CROSS_HEADS, CROSS_HEAD_DIM))
        qc = _proj(h, w_cq[i].astype(BF16), gain=norm_cross[i])
        cp = _cross_attend(qc, mk, mv, 0, batch, seq)
        cs_ = _cross_attend(qc, cache_mem_k[i].reshape(n, mem_len, d).astype(BF16),
                            cache_mem_v[i].reshape(n, mem_len, d).astype(BF16), tp, n, t)
        h = _proj(jnp.concatenate([cp, cs_], axis=0), w_co[i].astype(BF16), res=h)

        if i % 2 == 0:
            h = _ffn(h, norm_ffn[i], w_ffn_gu[a].astype(BF16), w_ffn_down[a].astype(BF16))
        else:
            combine = _router(h, norm_ffn[i], w_router[a], b_router[a])
            h = _moe(h, norm_ffn[i], combine, w_exp_gu[a].astype(BF16), w_exp_down[a].astype(BF16))

    y_prompt = _final_norm(h, norm_final, 0, tp).reshape(batch, seq, d)
    y_sample = _final_norm(h, norm_final, tp, ts).reshape(n, t, d)
    return (y_prompt, y_sample,
            jnp.stack(win_kp), jnp.stack(win_vp), jnp.stack(win_ks), jnp.stack(win_vs),
            jnp.stack(moba_kp), jnp.stack(moba_vp), jnp.stack(moba_ks), jnp.stack(moba_vs),
            jnp.stack(mem_kp), jnp.stack(mem_vp))
```
